```python
import math
import jax, jax.numpy as jnp
from jax import lax
import numpy as np

D_MODEL = 1024
BATCH = 2
SEQ = 8192
DEPTH = 4
DEC_BATCH = 128
DEC_SEQ = 4
PAST_LEN = 2048
PAGE_SIZE = 128

M_HEADS = 4
M_DH = 128
M_INNER = M_HEADS * M_DH
M_CHUNK = 64
A_HEADS = 8
A_DH = 64
A_INNER = A_HEADS * A_DH
MOBA_BLOCK = 256
MOBA_TOPK = 3
MOBA_QBLOCK = 64
ROPE_THETA = 500000.0
ROPE_DIMS = A_DH // 4
S_HEADS = 8
S_DH = 64
S_INNER = S_HEADS * S_DH
S_GROUPS = 2
S_STATE = 128
S_CONV_DIM = S_INNER + 2 * S_GROUPS * S_STATE
S_CHUNK = 128
CONV_W = 4
DEEPNORM_ALPHA = float((2 * DEPTH) ** 0.25)
DEEPNORM_BETA = float((8 * DEPTH) ** -0.25)
LN_EPS = 1e-5
NEG_BIG = -1e30
IN_SPLITS = (M_INNER, M_INNER, M_HEADS, M_HEADS, A_INNER, A_INNER, A_INNER, A_INNER,
             S_CONV_DIM, S_INNER, S_HEADS, 3 * D_MODEL)
IN_DIM = sum(IN_SPLITS)
IN_OFFSETS = tuple(int(v) for v in np.cumsum(IN_SPLITS)[:-1])

kernel_name = 'mlstm_moba_ssd_gated_hybrid_step'

F32 = jnp.float32


def layer_norm(x, g, b):
    xf = x.astype(F32)
    mu = xf.mean(-1, keepdims=True)
    var = jnp.mean(jnp.square(xf - mu), -1, keepdims=True)
    return ((xf - mu) * lax.rsqrt(var + LN_EPS) * g.astype(F32) + b.astype(F32)).astype(x.dtype)


def head_norm(h):
    mu = h.mean(-1, keepdims=True)
    var = jnp.mean(jnp.square(h - mu), -1, keepdims=True)
    return (h - mu) * lax.rsqrt(var + 1e-6)


def rms_norm(x, w):
    return x * lax.rsqrt(jnp.mean(jnp.square(x), -1, keepdims=True) + 1e-5) * w.astype(F32)


def causal_conv(x, state, w, b):
    t = x.shape[1]
    xe = jnp.concatenate([state.astype(x.dtype), x], axis=1)
    out = b + sum(xe[:, j:j + t] * w[j] for j in range(CONV_W))
    return out, xe[:, t:]


def rotary(x, pos):
    half = ROPE_DIMS // 2
    inv = jnp.power(ROPE_THETA, -jnp.arange(half, dtype=F32) * 2.0 / ROPE_DIMS)
    ang = pos.astype(F32)[:, None] * inv[None, :]
    cos = jnp.cos(ang)[None, :, None, :]
    sin = jnp.sin(ang)[None, :, None, :]
    x1 = x[..., :half].astype(F32)
    x2 = x[..., half:ROPE_DIMS].astype(F32)
    rot = jnp.concatenate([x1 * cos - x2 * sin, x2 * cos + x1 * sin], axis=-1).astype(x.dtype)
    return jnp.concatenate([rot, x[..., ROPE_DIMS:]], axis=-1)


def mlstm_chunkwise(q, k, v, logi, logf, c0, n0, m0, chunk):
    bsz, t, h, d = q.shape
    nc = t // chunk

    def to_chunks(a):
        return a.reshape((bsz, nc, chunk) + a.shape[2:]).swapaxes(0, 1)

    causal = jnp.tril(jnp.ones((chunk, chunk), dtype=bool))[None, :, :, None]

    def step(carry, inp):
        c, n, m = carry
        qc, kc, vc, ic, fc = inp
        b = jnp.cumsum(fc, axis=1)
        d_mat = jnp.where(causal, b[:, :, None, :] - b[:, None, :, :] + ic[:, None, :, :], -jnp.inf)
        inter = b + m[:, None, :]
        m_t = jnp.maximum(inter, d_mat.max(axis=2))
        w_intra = jnp.exp(d_mat - m_t[:, :, None, :])
        w_inter = jnp.exp(inter - m_t)
        sqk = jnp.einsum('bthd,bshd->btsh', qc, kc) * w_intra
        num = jnp.einsum('btsh,bshe->bthe', sqk, vc) + w_inter[..., None] * jnp.einsum('bthd,bhde->bthe', qc, c)
        den = sqk.sum(axis=2) + w_inter * jnp.einsum('bthd,bhd->bth', qc, n)
        hc = num / jnp.maximum(jnp.abs(den), jnp.exp(-m_t))[..., None]
        b_end = b[:, -1]
        g = b_end[:, None, :] - b + ic
        m_new = jnp.maximum(b_end + m, g.max(axis=1))
        w_k = jnp.exp(g - m_new[:, None, :])
        w_prev = jnp.exp(b_end + m - m_new)
        c_new = w_prev[..., None, None] * c + jnp.einsum('bsh,bshd,bshe->bhde', w_k, kc, vc)
        n_new = w_prev[..., None] * n + jnp.einsum('bsh,bshd->bhd', w_k, kc)
        return (c_new, n_new, m_new), hc

    (c, n, m), hs = lax.scan(step, (c0, n0, m0),
                             (to_chunks(q), to_chunks(k), to_chunks(v), to_chunks(logi), to_chunks(logf)))
    return hs.swapaxes(0, 1).reshape(bsz, t, h, d), c, n, m


def ssd_chunked(x, dt, a, bm, cm, s0, chunk):
    bsz, t, h, p = x.shape
    nc = t // chunk

    def to_chunks(arr):
        return arr.reshape((bsz, nc, chunk) + arr.shape[2:]).swapaxes(0, 1)

    causal = jnp.tril(jnp.ones((chunk, chunk), dtype=bool))[None, :, :, None]

    def step(s, inp):
        xc, dtc, bc, cc = inp
        acs = jnp.cumsum(dtc * a, axis=1)
        decay = jnp.exp(jnp.where(causal, acs[:, :, None, :] - acs[:, None, :, :], -jnp.inf))
        scores = jnp.einsum('bthn,bshn->btsh', cc, bc) * decay
        y = jnp.einsum('btsh,bsh,bshp->bthp', scores, dtc, xc)
        y = y + jnp.exp(acs)[..., None] * jnp.einsum('bthn,bhpn->bthp', cc, s)
        w_end = jnp.exp(acs[:, -1:, :] - acs) * dtc
        s_new = jnp.exp(acs[:, -1])[..., None, None] * s + jnp.einsum('bsh,bshn,bshp->bhpn', w_end, bc, xc)
        return s_new, y

    s, ys = lax.scan(step, s0, (to_chunks(x), to_chunks(dt), to_chunks(bm), to_chunks(cm)))
    return ys.swapaxes(0, 1).reshape(bsz, t, h, p), s


def moba_attention(q, k, v, q_pos, q_block):
    bsz, tq, h, dh = q.shape
    l = k.shape[1]
    nb = -(-l // MOBA_BLOCK)
    pad = ((0, 0), (0, nb * MOBA_BLOCK - l), (0, 0), (0, 0))
    kb = jnp.pad(k, pad).reshape(bsz, nb, MOBA_BLOCK, h, dh).transpose(0, 3, 1, 2, 4)
    vb = jnp.pad(v, pad).reshape(bsz, nb, MOBA_BLOCK, h, dh).transpose(0, 3, 1, 2, 4)
    k_mean = kb.astype(F32).mean(axis=3)
    k_sel = min(MOBA_TOPK, nb)
    scale = dh ** -0.5
    bi = jnp.arange(bsz)[:, None, None, None]
    hi = jnp.arange(h)[None, None, :, None]

    def one_block(args):
        qc, pc = args
        qb = qc.shape[1]
        qf = qc.astype(F32)
        own = pc // MOBA_BLOCK
        bscore = jnp.einsum('bqhd,bhnd->bqhn', qf, k_mean)
        past = jnp.arange(nb)[None, :] < own[:, None]
        bscore = jnp.where(past[None, :, None, :], bscore, NEG_BIG)
        _, sel = lax.top_k(bscore, k_sel)
        idx = jnp.concatenate([sel, jnp.broadcast_to(own[None, :, None, None], (bsz, qb, h, 1))], axis=-1)
        kg = kb[bi, hi, idx]
        vg = vb[bi, hi, idx]
        s = jnp.einsum('bqhd,bqhjsd->bqhjs', qf, kg.astype(F32)) * scale
        sel_ok = jnp.arange(k_sel)[None, :] < own[:, None]
        key_pos = own[:, None] * MOBA_BLOCK + jnp.arange(MOBA_BLOCK)[None, :]
        own_ok = key_pos <= pc[:, None]
        ok = jnp.concatenate([jnp.broadcast_to(sel_ok[:, :, None], (qb, k_sel, MOBA_BLOCK)),
                              own_ok[:, None, :]], axis=1)
        s = jnp.where(ok[None, :, None], s, NEG_BIG)
        p = jax.nn.softmax(s.reshape(bsz, qb, h, -1), axis=-1).reshape(s.shape)
        return jnp.einsum('bqhjs,bqhjsd->bqhd', p, vg.astype(F32)).astype(q.dtype)

    nq = tq // q_block
    qc = q.reshape(bsz, nq, q_block, h, dh).swapaxes(0, 1)
    pc = q_pos.reshape(nq, q_block)
    out = lax.map(one_block, (qc, pc))
    return out.swapaxes(0, 1).reshape(bsz, tq, h, dh)


def hybrid_layer(x, mconv0, c0, n0, m0, k_past, v_past, s0, sconv0,
                 w_in, m_conv_w, m_conv_b, m_wq, m_wk, m_wv, m_bi, m_bf, m_norm_w, m_skip,
                 s_conv_w, s_conv_b, s_dt_bias, s_a_log, s_d, s_norm_w,
                 w_ba, w_bb, w_bc, w_out, ln_g, ln_b, m_chunk, s_chunk, q_block):
    bsz, t, _ = x.shape
    proj = jnp.einsum('btd,de->bte', x, w_in)
    (m_x, m_z, m_i, m_f, a_q, a_k, a_v, a_z, s_xbc, s_z, s_dt, gates) = jnp.split(proj, IN_OFFSETS, axis=-1)

    xc, mconv = causal_conv(m_x, mconv0, m_conv_w, m_conv_b)
    xc = jax.nn.silu(xc).astype(F32)
    xch = xc.reshape(bsz, t, M_HEADS, M_DH)
    xmh = m_x.astype(F32).reshape(bsz, t, M_HEADS, M_DH)
    q = jnp.einsum('bthd,hde->bthe', xch, m_wq.astype(F32))
    k = jnp.einsum('bthd,hde->bthe', xch, m_wk.astype(F32)) * (M_DH ** -0.5)
    v = jnp.einsum('bthd,hde->bthe', xmh, m_wv.astype(F32))
    logi = (m_i + m_bi).astype(F32)
    logf = jax.nn.log_sigmoid((m_f + m_bf).astype(F32))
    hm, c, n, m = mlstm_chunkwise(q, k, v, logi, logf, c0.astype(F32), n0.astype(F32), m0.astype(F32), m_chunk)
    hn = head_norm(hm).reshape(bsz, t, M_INNER) * m_norm_w.astype(F32)
    y_a = ((hn + m_skip.astype(F32) * xc) * jax.nn.silu(m_z.astype(F32))).astype(x.dtype)

    pos = k_past.shape[1] + jnp.arange(t, dtype=jnp.int32)
    qa = rotary(a_q.reshape(bsz, t, A_HEADS, A_DH), pos)
    ka = rotary(a_k.reshape(bsz, t, A_HEADS, A_DH), pos)
    va = a_v.reshape(bsz, t, A_HEADS, A_DH)
    k_all = jnp.concatenate([k_past.astype(ka.dtype), ka], axis=1)
    v_all = jnp.concatenate([v_past.astype(va.dtype), va], axis=1)
    o = moba_attention(qa, k_all, v_all, pos, q_block)
    y_b = o.reshape(bsz, t, A_INNER) * jax.nn.silu(a_z)

    xbc, sconv = causal_conv(s_xbc, sconv0, s_conv_w, s_conv_b)
    xbc = jax.nn.silu(xbc).astype(F32)
    xs, bm, cm = jnp.split(xbc, (S_INNER, S_INNER + S_GROUPS * S_STATE), axis=-1)
    xs = xs.reshape(bsz, t, S_HEADS, S_DH)
    rep = S_HEADS // S_GROUPS
    bm = jnp.repeat(bm.reshape(bsz, t, S_GROUPS, S_STATE), rep, axis=2)
    cm = jnp.repeat(cm.reshape(bsz, t, S_GROUPS, S_STATE), rep, axis=2)
    dt = jax.nn.softplus((s_dt + s_dt_bias).astype(F32))
    a = -jnp.exp(s_a_log.astype(F32))
    ys, s_state = ssd_chunked(xs, dt, a, bm, cm, s0.astype(F32), s_chunk)
    ys = ys + s_d.astype(F32)[:, None] * xs
    y_c = rms_norm(ys.reshape(bsz, t, S_INNER) * jax.nn.silu(s_z.astype(F32)), s_norm_w).astype(x.dtype)

    g_a, g_b, g_c = jnp.split(jax.nn.sigmoid(gates), 3, axis=-1)
    merged = g_a * (y_a @ w_ba) + g_b * (y_b @ w_bb) + g_c * (y_c @ w_bc)
    out = merged @ w_out
    x_new = layer_norm(DEEPNORM_ALPHA * x + out, ln_g, ln_b)
    return x_new, (ka, va, c, n, m, mconv, s_state, sconv)


def setup_inputs(seed: int = 0) -> dict:
    key = jax.random.key(seed)
    keys = iter(list(jax.random.split(key, 40)))

    def nrm(shape, scale):
        return jax.random.normal(next(keys), shape, F32) * scale

    n_pages = PAST_LEN // PAGE_SIZE
    n_used = DEC_BATCH * n_pages
    n_pool = n_used + n_used // 4
    x_prompt = nrm((BATCH, SEQ, D_MODEL), 1.0)
    x_sample = nrm((DEC_BATCH, DEC_SEQ, D_MODEL), 1.0)
    cache_k = nrm((DEPTH, n_pool, PAGE_SIZE, A_HEADS, A_DH), 1.0)
    cache_v = nrm((DEPTH, n_pool, PAGE_SIZE, A_HEADS, A_DH), 1.0)
    page_table = jax.random.permutation(next(keys), n_pool)[:n_used].reshape(DEC_BATCH, n_pages).astype(jnp.int32)
    state_mlstm_c = nrm((DEPTH, DEC_BATCH, M_HEADS, M_DH, M_DH), 0.1)
    state_mlstm_n = nrm((DEPTH, DEC_BATCH, M_HEADS, M_DH), 0.5)
    state_mlstm_m = nrm((DEPTH, DEC_BATCH, M_HEADS), 0.5)
    state_mlstm_conv = nrm((DEPTH, DEC_BATCH, CONV_W - 1, M_INNER), 1.0)
    state_ssm = nrm((DEPTH, DEC_BATCH, S_HEADS, S_DH, S_STATE), 0.1)
    state_ssm_conv = nrm((DEPTH, DEC_BATCH, CONV_W - 1, S_CONV_DIM), 1.0)

    w_in = nrm((DEPTH, D_MODEL, IN_DIM), D_MODEL ** -0.5)
    mlstm_conv_w = nrm((DEPTH, CONV_W, M_INNER), CONV_W ** -0.5)
    mlstm_conv_b = nrm((DEPTH, M_INNER), 0.01)
    mlstm_wq = nrm((DEPTH, M_HEADS, M_DH, M_DH), M_DH ** -0.5)
    mlstm_wk = nrm((DEPTH, M_HEADS, M_DH, M_DH), M_DH ** -0.5)
    mlstm_wv = nrm((DEPTH, M_HEADS, M_DH, M_DH), M_DH ** -0.5)
    mlstm_b_i = nrm((DEPTH, M_HEADS), 0.1)
    mlstm_b_f = jnp.linspace(3.0, 6.0, M_HEADS, dtype=F32)[None, :] + nrm((DEPTH, M_HEADS), 0.1)
    mlstm_norm_w = 1.0 + nrm((DEPTH, M_INNER), 0.02)
    mlstm_skip = 1.0 + nrm((DEPTH, M_INNER), 0.02)
    ssm_conv_w = nrm((DEPTH, CONV_W, S_CONV_DIM), CONV_W ** -0.5)
    ssm_conv_b = nrm((DEPTH, S_CONV_DIM), 0.01)
    dt0 = jnp.exp(jax.random.uniform(next(keys), (DEPTH, S_HEADS), F32, math.log(1e-3), math.log(1e-1)))
    ssm_dt_bias = dt0 + jnp.log(-jnp.expm1(-dt0))
    ssm_a_log = jnp.log(jax.random.uniform(next(keys), (DEPTH, S_HEADS), F32, 1.0, 16.0))
    ssm_d = 1.0 + nrm((DEPTH, S_HEADS), 0.02)
    ssm_norm_w = 1.0 + nrm((DEPTH, S_INNER), 0.02)
    w_branch_a = nrm((DEPTH, M_INNER, D_MODEL), M_INNER ** -0.5 * DEEPNORM_BETA)
    w_branch_b = nrm((DEPTH, A_INNER, D_MODEL), A_INNER ** -0.5 * DEEPNORM_BETA)
    w_branch_c = nrm((DEPTH, S_INNER, D_MODEL), S_INNER ** -0.5 * DEEPNORM_BETA)
    w_out = nrm((DEPTH, D_MODEL, D_MODEL), D_MODEL ** -0.5 * DEEPNORM_BETA)
    ln_g = 1.0 + nrm((DEPTH, D_MODEL), 0.02)
    ln_b = nrm((DEPTH, D_MODEL), 0.02)
    return {'x_prompt': x_prompt, 'x_sample': x_sample, 'cache_k': cache_k, 'cache_v': cache_v,
            'page_table': page_table, 'state_mlstm_c': state_mlstm_c, 'state_mlstm_n': state_mlstm_n,
            'state_mlstm_m': state_mlstm_m, 'state_mlstm_conv': state_mlstm_conv, 'state_ssm': state_ssm,
            'state_ssm_conv': state_ssm_conv, 'w_in': w_in, 'mlstm_conv_w': mlstm_conv_w,
            'mlstm_conv_b': mlstm_conv_b, 'mlstm_wq': mlstm_wq, 'mlstm_wk': mlstm_wk, 'mlstm_wv': mlstm_wv,
            'mlstm_b_i': mlstm_b_i, 'mlstm_b_f': mlstm_b_f, 'mlstm_norm_w': mlstm_norm_w, 'mlstm_skip': mlstm_skip,
            'ssm_conv_w': ssm_conv_w, 'ssm_conv_b': ssm_conv_b, 'ssm_dt_bias': ssm_dt_bias, 'ssm_a_log': ssm_a_log,
            'ssm_d': ssm_d, 'ssm_norm_w': ssm_norm_w, 'w_branch_a': w_branch_a, 'w_branch_b': w_branch_b,
            'w_branch_c': w_branch_c, 'w_out': w_out, 'ln_g': ln_g, 'ln_b': ln_b}


def reference(x_prompt, x_sample, cache_k, cache_v, page_table, state_mlstm_c, state_mlstm_n, state_mlstm_m,
              state_mlstm_conv, state_ssm, state_ssm_conv, w_in, mlstm_conv_w, mlstm_conv_b, mlstm_wq, mlstm_wk,
              mlstm_wv, mlstm_b_i, mlstm_b_f, mlstm_norm_w, mlstm_skip, ssm_conv_w, ssm_conv_b, ssm_dt_bias,
              ssm_a_log, ssm_d, ssm_norm_w, w_branch_a, w_branch_b, w_branch_c, w_out, ln_g, ln_b):
    x_p = x_prompt
    x_s = x_sample
    bp = x_p.shape[0]
    bs, ts = x_s.shape[0], x_s.shape[1]
    n_past = page_table.shape[1] * cache_k.shape[2]
    z_mconv = jnp.zeros((bp, CONV_W - 1, M_INNER), x_p.dtype)
    z_c = jnp.zeros((bp, M_HEADS, M_DH, M_DH), F32)
    z_n = jnp.zeros((bp, M_HEADS, M_DH), F32)
    z_m = jnp.zeros((bp, M_HEADS), F32)
    z_kv = jnp.zeros((bp, 0, A_HEADS, A_DH), x_p.dtype)
    z_s = jnp.zeros((bp, S_HEADS, S_DH, S_STATE), F32)
    z_sconv = jnp.zeros((bp, CONV_W - 1, S_CONV_DIM), x_p.dtype)
    new_p = []
    new_s = []
    for l in range(DEPTH):
        w_l = (w_in[l], mlstm_conv_w[l], mlstm_conv_b[l], mlstm_wq[l], mlstm_wk[l], mlstm_wv[l], mlstm_b_i[l],
               mlstm_b_f[l], mlstm_norm_w[l], mlstm_skip[l], ssm_conv_w[l], ssm_conv_b[l], ssm_dt_bias[l],
               ssm_a_log[l], ssm_d[l], ssm_norm_w[l], w_branch_a[l], w_branch_b[l], w_branch_c[l], w_out[l],
               ln_g[l], ln_b[l])
        x_p, st_p = hybrid_layer(x_p, z_mconv, z_c, z_n, z_m, z_kv, z_kv, z_s, z_sconv, *w_l,
                                 M_CHUNK, S_CHUNK, MOBA_QBLOCK)
        new_p.append(st_p)
        k_past = cache_k[l, page_table].reshape(bs, n_past, A_HEADS, A_DH)
        v_past = cache_v[l, page_table].reshape(bs, n_past, A_HEADS, A_DH)
        x_s, st_s = hybrid_layer(x_s, state_mlstm_conv[l], state_mlstm_c[l], state_mlstm_n[l], state_mlstm_m[l],
                                 k_past, v_past, state_ssm[l], state_ssm_conv[l], *w_l, ts, ts, 1)
        new_s.append(st_s)
    (k_p, v_p, mc_p, mn_p, mm_p, mconv_p, ssm_p, sconv_p) = [jnp.stack(a, axis=0) for a in zip(*new_p)]
    (k_s, v_s, mc_s, mn_s, mm_s, mconv_s, ssm_s, sconv_s) = [jnp.stack(a, axis=0) for a in zip(*new_s)]
    return (x_p, x_s, k_p, v_p, mc_p, mn_p, mm_p, mconv_p, ssm_p, sconv_p,
            k_s, v_s, mc_s, mn_s, mm_s, mconv_s, ssm_s, sconv_s)
```

```python
import functools
import math

import numpy as np
import jax
import jax.numpy as jnp
from jax import lax
from jax.experimental import pallas as pl
from jax.experimental.pallas import tpu as pltpu

F32 = jnp.float32
BF16 = jnp.bfloat16

D_MODEL = 1024
M_HEADS, M_DH = 4, 128
M_INNER = M_HEADS * M_DH
A_HEADS, A_DH = 8, 64
A_INNER = A_HEADS * A_DH
MOBA_BLOCK = 256
MOBA_TOPK = 3
ROPE_THETA = 500000.0
ROPE_DIMS = A_DH // 4
S_HEADS, S_DH = 8, 64
S_INNER = S_HEADS * S_DH
S_GROUPS, S_STATE = 2, 128
S_CONV_DIM = S_INNER + 2 * S_GROUPS * S_STATE
CONV_W = 4
DEPTH_ALPHA_POW = 0.25
LN_EPS = 1e-5
NEG = -1e30

IN_SPLITS = (M_INNER, M_INNER, M_HEADS, M_HEADS, A_INNER, A_INNER, A_INNER, A_INNER,
             S_CONV_DIM, S_INNER, S_HEADS, 3 * D_MODEL)
IN_OFFSETS = tuple(int(v) for v in np.cumsum((0,) + IN_SPLITS)[:-1])

PROJ_MAIN = 3 * D_MODEL + S_CONV_DIM + 2 * M_INNER + 4 * A_INNER + S_INNER
COL_MX, COL_MZ, COL_AQ, COL_AK, COL_AV, COL_AZ, COL_SZ = 8, 9, 10, 11, 12, 13, 14
SMALL_W = 128
LANE = 128
SUB = 8
VMEM_LIMIT = 56 * 1024 * 1024


def _dot(a, b):
    return jnp.dot(a, b, preferred_element_type=F32)


def _dot_nt(a, b):
    return lax.dot_general(a, b, (((1,), (1,)), ((), ())), preferred_element_type=F32)


def _dot_tn(a, b):
    return lax.dot_general(a, b, (((0,), (0,)), ((), ())), preferred_element_type=F32)


def _split3(x):
    h = x.astype(BF16)
    r = x - h.astype(F32)
    m = r.astype(BF16)
    lo = (r - m.astype(F32)).astype(BF16)
    return h, m, lo


def _cumsum_rows(tril_bf, x):
    h, m, lo = _split3(x)
    return (_dot(tril_bf, lo) + _dot(tril_bf, m)) + _dot(tril_bf, h)


def _cumsum_cols(xt, triu_bf):
    h, m, lo = _split3(xt)
    return (_dot(lo, triu_bf) + _dot(m, triu_bf)) + _dot(h, triu_bf)


def _dot_hi(a, b, dot):
    ah = a.astype(BF16)
    al = (a - ah.astype(F32)).astype(BF16)
    bh = b.astype(BF16)
    bl = (b - bh.astype(F32)).astype(BF16)
    return (dot(al, bh) + dot(ah, bl)) + dot(ah, bh)


def _dot_nt_hi(a, b):
    return _dot_hi(a, b, _dot_nt)


def _silu(x):
    return x * jax.nn.sigmoid(x)


def _tri(n, lower):
    r = lax.broadcasted_iota(jnp.int32, (n, n), 0)
    c = lax.broadcasted_iota(jnp.int32, (n, n), 1)
    return (r >= c) if lower else (r <= c)


def _cparams(sem):
    return pltpu.CompilerParams(dimension_semantics=sem, vmem_limit_bytes=VMEM_LIMIT)


def _proj_kernel(x_ref, w_ref, ws_ref, o_ref, os_ref, xb_ref):
    @pl.when(pl.program_id(1) == 0)
    def _():
        xb = x_ref[...].astype(BF16)
        xb_ref[...] = xb
        os_ref[...] = _dot(xb, ws_ref[...])

    o_ref[...] = _dot(xb_ref[...], w_ref[...])


def _proj(x, w, ws, tm, tn):
    n = x.shape[0]
    return pl.pallas_call(
        _proj_kernel,
        grid=(n // tm, PROJ_MAIN // tn),
        in_specs=[pl.BlockSpec((tm, D_MODEL), lambda i, j: (i, 0)),
                  pl.BlockSpec((D_MODEL, tn), lambda i, j: (0, j)),
                  pl.BlockSpec((D_MODEL, SMALL_W), lambda i, j: (0, 0))],
        out_specs=[pl.BlockSpec((tm, tn), lambda i, j: (i, j)),
                   pl.BlockSpec((tm, SMALL_W), lambda i, j: (i, 0))],
        out_shape=[jax.ShapeDtypeStruct((n, PROJ_MAIN), F32),
                   jax.ShapeDtypeStruct((n, SMALL_W), F32)],
        scratch_shapes=[pltpu.VMEM((tm, D_MODEL), BF16)],
        compiler_params=_cparams(("arbitrary", "arbitrary")),
        name="in_proj",
    )(x, w, ws)


def _conv_silu(xe_ref, x, cw_ref, cb_ref):
    length = x.shape[0]
    xe_ref[SUB:SUB + length, :] = x
    acc = cb_ref[...] + xe_ref[5:5 + length, :] * cw_ref[0:1, :]
    for j in range(1, CONV_W):
        acc = acc + xe_ref[5 + j:5 + j + length, :] * cw_ref[j:j + 1, :]
    return _silu(acc)


def _mlstm_kernel(tvalid, mx_ref, mz_ref, sm_ref, conv0_ref, c0_ref, n0_ref, m0_ref,
                  cw_ref, cb_ref, wqk_ref, wv_ref, gb_ref, nw_ref, sk_ref,
                  ya_ref, ct_ref, nt_ref, mt_ref, convt_ref,
                  xe_ref, c_ref, n_ref, m_ref):
    ci = pl.program_id(1)
    nchunk = pl.num_programs(1)
    length = mx_ref.shape[0]
    nh, dh = c_ref.shape[0], c_ref.shape[1]

    @pl.when(ci == 0)
    def _():
        xe_ref[5:8, :] = conv0_ref[...]
        c_ref[...] = c0_ref[...]
        n_ref[...] = n0_ref[...]
        m_ref[...] = m0_ref[...]

    mx = mx_ref[...]
    xc = _conv_silu(xe_ref, mx, cw_ref, cb_ref)
    new_tail = xe_ref[5 + tvalid:8 + tvalid, :]
    xe_ref[5:8, :] = new_tail

    z = sm_ref[...] + gb_ref[...]
    lane = lax.broadcasted_iota(jnp.int32, z.shape, 1)
    ls = jnp.minimum(z, 0.0) - jnp.log1p(jnp.exp(-jnp.abs(z)))
    g = jnp.where(lane < nh, z, ls)
    if tvalid < length:
        row = lax.broadcasted_iota(jnp.int32, z.shape, 0)
        g = jnp.where(row < tvalid, g, jnp.where(lane < nh, NEG, 0.0))
    gt = g.T
    tril = _tri(length, True)
    tril_bf = jnp.where(tril, 1.0, 0.0).astype(BF16)
    triu_bf = jnp.where(_tri(length, False), 1.0, 0.0).astype(BF16)
    b_col = _cumsum_rows(tril_bf, g)
    b_row = _cumsum_cols(gt, triu_bf)

    m_row = m_ref[...]
    lane_row = lax.broadcasted_iota(jnp.int32, m_row.shape, 1)
    m_new_row = m_row
    mz = mz_ref[...]
    for h in range(nh):
        sl = slice(h * dh, (h + 1) * dh)
        xch = xc[:, sl]
        qk = _dot(xch.astype(BF16), wqk_ref[h])
        q = qk[:, :dh]
        k = qk[:, dh:] * (dh ** -0.5)
        v = _dot(mx[:, sl].astype(BF16), wv_ref[h])
        qb, kb, vb = q.astype(BF16), k.astype(BF16), v.astype(BF16)

        bc = b_col[:, nh + h:nh + h + 1]
        br = b_row[nh + h:nh + h + 1, :]
        ic = g[:, h:h + 1]
        ir = gt[h:h + 1, :]
        mp = m_row[:, h:h + 1]
        bend = br[:, length - 1:length]

        dm = jnp.where(tril, bc + (ir - br), NEG)
        inter = bc + mp
        m_t = jnp.maximum(inter, jnp.max(dm, axis=1, keepdims=True))
        w_intra = jnp.exp(dm - m_t)
        w_inter = jnp.exp(inter - m_t)
        sqk = _dot_nt(qb, kb) * w_intra
        c_old = c_ref[h]
        n_old = n_ref[h:h + 1, :]
        num = _dot(sqk.astype(BF16), vb) + w_inter * _dot(qb, c_old.astype(BF16))
        den = jnp.sum(sqk, axis=1, keepdims=True) + w_inter * jnp.sum(q * n_old, axis=1, keepdims=True)
        hc = num / jnp.maximum(jnp.abs(den), jnp.exp(-m_t))
        mu = jnp.mean(hc, axis=1, keepdims=True)
        hcc = hc - mu
        var = jnp.mean(hcc * hcc, axis=1, keepdims=True)
        hn = hcc * lax.rsqrt(var + 1e-6)
        ya = (hn * nw_ref[:, sl] + sk_ref[:, sl] * xch) * _silu(mz[:, sl])
        ya_ref[:, sl] = ya.astype(ya_ref.dtype)

        g_c = (bend - bc) + ic
        g_r = (bend - br) + ir
        m_new = jnp.maximum(bend + mp, jnp.max(g_r, axis=1, keepdims=True))
        w_k = jnp.exp(g_c - m_new)
        w_prev = jnp.exp(bend + mp - m_new)
        kw = k * w_k
        c_ref[h] = w_prev * c_old + _dot_tn(kw.astype(BF16), vb)
        n_ref[h:h + 1, :] = w_prev * n_old + jnp.sum(kw, axis=0, keepdims=True)
        m_new_row = jnp.where(lane_row == h, m_new, m_new_row)
    m_ref[...] = m_new_row

    @pl.when(ci == nchunk - 1)
    def _():
        ct_ref[...] = c_ref[...]
        nt_ref[...] = n_ref[...]
        mt_ref[...] = m_new_row
        convt_ref[...] = new_tail


def _mlstm(proj, small, conv0, c0, n0, m0p, wl, nbatch, length, tvalid):
    n = proj.shape[0]
    nchunk = n // (nbatch * length)
    nh, dh = M_HEADS, M_DH
    row = lambda b, c: (b * nchunk + c)
    full = lambda shp: pl.BlockSpec(shp, lambda b, c: (0,) * len(shp))
    per_b = lambda shp: pl.BlockSpec((None,) + shp, lambda b, c: (b,) + (0,) * len(shp))
    return pl.pallas_call(
        functools.partial(_mlstm_kernel, tvalid),
        grid=(nbatch, nchunk),
        in_specs=[pl.BlockSpec((length, M_INNER), lambda b, c: (row(b, c), COL_MX)),
                  pl.BlockSpec((length, M_INNER), lambda b, c: (row(b, c), COL_MZ)),
                  pl.BlockSpec((length, SMALL_W), lambda b, c: (row(b, c), 0)),
                  per_b((CONV_W - 1, M_INNER)), per_b((nh, dh, dh)), per_b((nh, dh)), per_b((1, LANE)),
                  full((CONV_W, M_INNER)), full((1, M_INNER)), full((nh, dh, 2 * dh)), full((nh, dh, dh)),
                  full((1, LANE)), full((1, M_INNER)), full((1, M_INNER))],
        out_specs=[pl.BlockSpec((length, M_INNER), lambda b, c: (row(b, c), 0)),
                   per_b((nh, dh, dh)), per_b((nh, dh)), per_b((1, LANE)), per_b((CONV_W - 1, M_INNER))],
        out_shape=[jax.ShapeDtypeStruct((n, M_INNER), BF16),
                   jax.ShapeDtypeStruct((nbatch, nh, dh, dh), F32),
                   jax.ShapeDtypeStruct((nbatch, nh, dh), F32),
                   jax.ShapeDtypeStruct((nbatch, 1, LANE), F32),
                   jax.ShapeDtypeStruct((nbatch, CONV_W - 1, M_INNER), F32)],
        scratch_shapes=[pltpu.VMEM((length + 2 * SUB, M_INNER), F32),
                        pltpu.VMEM((nh, dh, dh), F32), pltpu.VMEM((nh, dh), F32), pltpu.VMEM((1, LANE), F32)],
        compiler_params=_cparams(("arbitrary", "arbitrary")),
        name="mlstm",
    )(proj, proj, small, conv0, c0, n0, m0p,
      wl["m_cw"], wl["m_cb"], wl["m_wqk"], wl["m_wv"], wl["gate_bias"], wl["m_nw"], wl["m_sk"])


def _ssd_kernel(tvalid, xbc_ref, sz_ref, sm_ref, conv0_ref, s0_ref,
                cw_ref, cb_ref, gb_ref, alog_ref, dsk_ref, nw_ref,
                yc_ref, st_ref, convt_ref,
                xe_ref, s_ref, y_ref):
    ci = pl.program_id(1)
    nchunk = pl.num_programs(1)
    length = xbc_ref.shape[0]
    nh, pd, ns = s_ref.shape
    rep = nh // S_GROUPS

    @pl.when(ci == 0)
    def _():
        xe_ref[5:8, :] = conv0_ref[...]
        s_ref[...] = s0_ref[...]

    xbc = _conv_silu(xe_ref, xbc_ref[...], cw_ref, cb_ref)
    new_tail = xe_ref[5 + tvalid:8 + tvalid, :]
    xe_ref[5:8, :] = new_tail
    xs = xbc[:, :S_INNER]
    bmat = xbc[:, S_INNER:S_INNER + S_GROUPS * ns]
    cmat = xbc[:, S_INNER + S_GROUPS * ns:]

    z = sm_ref[...] + gb_ref[...]
    dt = jnp.maximum(z, 0.0) + jnp.log1p(jnp.exp(-jnp.abs(z)))
    if tvalid < length:
        row = lax.broadcasted_iota(jnp.int32, z.shape, 0)
        dt = jnp.where(row < tvalid, dt, 0.0)
    a_row = -jnp.exp(alog_ref[...])
    da = dt * a_row
    tril = _tri(length, True)
    tril_bf = jnp.where(tril, 1.0, 0.0).astype(BF16)
    triu_bf = jnp.where(_tri(length, False), 1.0, 0.0).astype(BF16)
    acs_col = _cumsum_rows(tril_bf, da)
    acs_row = _cumsum_cols(da.T, triu_bf)
    dt_t = dt.T

    for grp in range(S_GROUPS):
        bg = bmat[:, grp * ns:(grp + 1) * ns].astype(BF16)
        cg = cmat[:, grp * ns:(grp + 1) * ns].astype(BF16)
        cb = _dot_nt(cg, bg)
        for hh in range(rep):
            h = grp * rep + hh
            ln = SUB + h
            ac = acs_col[:, ln:ln + 1]
            ar = acs_row[ln:ln + 1, :]
            dtr = dt_t[ln:ln + 1, :]
            dtc = dt[:, ln:ln + 1]
            aend = ar[:, length - 1:length]
            decay = jnp.exp(jnp.where(tril, ac - ar, NEG))
            scores = cb * decay * dtr
            xh = xs[:, h * pd:(h + 1) * pd]
            xhb = xh.astype(BF16)
            s_old = s_ref[h]
            y = _dot(scores.astype(BF16), xhb) + jnp.exp(ac) * _dot_nt(cg, s_old.astype(BF16))
            y_ref[:, h * pd:(h + 1) * pd] = y
            w_end = jnp.exp(aend - ac) * dtc
            s_ref[h] = jnp.exp(aend) * s_old + _dot_tn((xh * w_end).astype(BF16), bg)

    ys = y_ref[...] + dsk_ref[...] * xs
    gated = ys * _silu(sz_ref[...])
    ms = jnp.mean(gated * gated, axis=1, keepdims=True)
    yc_ref[...] = (gated * lax.rsqrt(ms + 1e-5) * nw_ref[...]).astype(yc_ref.dtype)

    @pl.when(ci == nchunk - 1)
    def _():
        st_ref[...] = s_ref[...]
        convt_ref[...] = new_tail


def _ssd(proj, small, conv0, s0, wl, nbatch, length, tvalid):
    n = proj.shape[0]
    nchunk = n // (nbatch * length)
    row = lambda b, c: (b * nchunk + c)
    full = lambda shp: pl.BlockSpec(shp, lambda b, c: (0,) * len(shp))
    per_b = lambda shp: pl.BlockSpec((None,) + shp, lambda b, c: (b,) + (0,) * len(shp))
    return pl.pallas_call(
        functools.partial(_ssd_kernel, tvalid),
        grid=(nbatch, nchunk),
        in_specs=[pl.BlockSpec((length, S_CONV_DIM), lambda b, c: (row(b, c), 3)),
                  pl.BlockSpec((length, S_INNER), lambda b, c: (row(b, c), COL_SZ)),
                  pl.BlockSpec((length, SMALL_W), lambda b, c: (row(b, c), 0)),
                  per_b((CONV_W - 1, S_CONV_DIM)), per_b((S_HEADS, S_DH, S_STATE)),
                  full((CONV_W, S_CONV_DIM)), full((1, S_CONV_DIM)), full((1, LANE)), full((1, LANE)),
                  full((1, S_INNER)), full((1, S_INNER))],
        out_specs=[pl.BlockSpec((length, S_INNER), lambda b, c: (row(b, c), 0)),
                   per_b((S_HEADS, S_DH, S_STATE)), per_b((CONV_W - 1, S_CONV_DIM))],
        out_shape=[jax.ShapeDtypeStruct((n, S_INNER), BF16),
                   jax.ShapeDtypeStruct((nbatch, S_HEADS, S_DH, S_STATE), F32),
                   jax.ShapeDtypeStruct((nbatch, CONV_W - 1, S_CONV_DIM), F32)],
        scratch_shapes=[pltpu.VMEM((length + 2 * SUB, S_CONV_DIM), F32),
                        pltpu.VMEM((S_HEADS, S_DH, S_STATE), F32),
                        pltpu.VMEM((length, S_INNER), F32)],
        compiler_params=_cparams(("arbitrary", "arbitrary")),
        name="ssd",
    )(proj, proj, small, conv0, s0,
      wl["s_cw"], wl["s_cb"], wl["gate_bias"], wl["s_alog"], wl["s_dskip"], wl["s_nw"])


def _rope(x, cos, sin):
    width = x.shape[1]
    lane = lax.broadcasted_iota(jnp.int32, x.shape, 1) % A_DH
    half = ROPE_DIMS // 2
    partner = jnp.where(lane < half, pltpu.roll(x, width - half, 1), pltpu.roll(x, half, 1))
    return x * cos + partner * sin


def _attn_prep_kernel(aq_ref, ak_ref, av_ref, cos_ref, sin_ref,
                      q_ref, kt_ref, vt_ref, ktb_ref, vtb_ref, km_ref):
    i = pl.program_id(1)
    cos = cos_ref[...]
    sin = sin_ref[...]
    q_ref[...] = _rope(aq_ref[...], cos, sin) * (A_DH ** -0.5)
    k = _rope(ak_ref[...], cos, sin)
    kt = k.T
    vt = av_ref[...].T
    kt_ref[...] = kt
    vt_ref[...] = vt
    ktb_ref[...] = kt.astype(BF16)
    vtb_ref[...] = vt.astype(BF16)
    km_ref[pl.ds(i, 1), :] = jnp.sum(k, axis=0, keepdims=True) * (1.0 / MOBA_BLOCK)


def _attn_prep(proj, cos, sin, nbatch):
    n = proj.shape[0]
    t = n // nbatch
    nt = t // MOBA_BLOCK
    blk = (MOBA_BLOCK, A_INNER)
    row = lambda b, i: b * nt + i
    tspec = pl.BlockSpec((None, A_INNER, MOBA_BLOCK), lambda b, i: (b, 0, i))
    return pl.pallas_call(
        _attn_prep_kernel,
        grid=(nbatch, nt),
        in_specs=[pl.BlockSpec(blk, lambda b, i: (row(b, i), COL_AQ)),
                  pl.BlockSpec(blk, lambda b, i: (row(b, i), COL_AK)),
                  pl.BlockSpec(blk, lambda b, i: (row(b, i), COL_AV)),
                  pl.BlockSpec(blk, lambda b, i: (i, 0)),
                  pl.BlockSpec(blk, lambda b, i: (i, 0))],
        out_specs=[pl.BlockSpec(blk, lambda b, i: (row(b, i), 0)),
                   tspec, tspec, tspec, tspec,
                   pl.BlockSpec((None, nt, A_INNER), lambda b, i: (b, 0, 0))],
        out_shape=[jax.ShapeDtypeStruct((n, A_INNER), F32),
                   jax.ShapeDtypeStruct((nbatch, A_INNER, t), F32),
                   jax.ShapeDtypeStruct((nbatch, A_INNER, t), F32),
                   jax.ShapeDtypeStruct((nbatch, A_INNER, t), BF16),
                   jax.ShapeDtypeStruct((nbatch, A_INNER, t), BF16),
                   jax.ShapeDtypeStruct((nbatch, nt, A_INNER), F32)],
        compiler_params=_cparams(("arbitrary", "arbitrary")),
        name="attn_prep",
    )(proj, proj, proj, cos, sin)


def _select_topk(scores, valid, nsel):
    nb = scores.shape[0]
    idx = lax.broadcasted_iota(jnp.int32, scores.shape, 0).astype(F32)
    s = jnp.where(valid, scores, -jnp.inf)
    sel = jnp.zeros(scores.shape, F32)
    for _ in range(nsel):
        mx = jnp.max(s, axis=0, keepdims=True)
        first = jnp.min(jnp.where(s == mx, idx, float(nb)), axis=0, keepdims=True)
        hit = idx == first
        sel = jnp.where(jnp.logical_and(hit, valid), 1.0, sel)
        s = jnp.where(hit, -jnp.inf, s)
    return sel > 0.5


def _attn_kernel(qi_tab, kb_tab, q_ref, kt_ref, vt_ref, km_ref, az_ref, yb_ref,
                 qa_ref, m_ref, acc_ref):
    p = pl.program_id(1)
    qi = qi_tab[p]
    kb = kb_tab[p]
    nblk = km_ref.shape[0]
    tq = q_ref.shape[0]
    tk = kt_ref.shape[1]
    dh = A_DH

    @pl.when(kb == qi)
    def _():
        q = q_ref[...]
        km = km_ref[...]
        blk = lax.broadcasted_iota(jnp.int32, (nblk, tq), 0)
        past = blk < qi
        zpad = jnp.zeros((LANE - nblk, tq), F32)
        for h in range(A_HEADS):
            sl = slice(h * dh, (h + 1) * dh)
            bs = _dot_nt_hi(km[:, sl], q[:, sl])
            keep = jnp.logical_or(_select_topk(bs, past, MOBA_TOPK), blk == qi)
            bias_t = jnp.concatenate([jnp.where(keep, 0.0, NEG), zpad], axis=0).T
            qa_ref[h] = jnp.concatenate([q[:, sl], bias_t[:, :dh]], axis=1).astype(BF16)
        m_ref[...] = jnp.full(m_ref.shape, NEG, F32)
        acc_ref[...] = jnp.zeros(acc_ref.shape, F32)

    def step(causal):
        rid = lax.broadcasted_iota(jnp.int32, (dh, tk), 0)
        onehot = jnp.where(rid == kb, 1.0, 0.0).astype(BF16)
        ones_row = jnp.where(rid == 0, 1.0, 0.0).astype(BF16)
        for h in range(A_HEADS):
            sl = slice(h * dh, (h + 1) * dh)
            kt_aug = jnp.concatenate([kt_ref[sl, :], onehot], axis=0)
            s = _dot(qa_ref[h], kt_aug)
            if causal is not None:
                s = jnp.where(causal, s, NEG)
            m_old = m_ref[:, h:h + 1]
            m_new = jnp.maximum(m_old, jnp.max(s, axis=1, keepdims=True))
            alpha = jnp.exp(m_old - m_new)
            pr = jnp.exp(s - m_new)
            vt_aug = jnp.concatenate([vt_ref[sl, :], ones_row], axis=0)
            acc_ref[h] = alpha * acc_ref[h] + _dot_nt(pr.astype(BF16), vt_aug)
            m_ref[:, h:h + 1] = m_new

    @pl.when(kb == qi)
    def _():
        qpos = lax.broadcasted_iota(jnp.int32, (tq, tk), 0)
        kpos = lax.broadcasted_iota(jnp.int32, (tq, tk), 1)
        step(kpos <= qpos)

    @pl.when(kb != qi)
    def _():
        step(None)

    @pl.when(jnp.logical_or(kb == qi - 1, qi == 0))
    def _():
        for h in range(A_HEADS):
            sl = slice(h * dh, (h + 1) * dh)
            a = acc_ref[h]
            o = a[:, :dh] / a[:, dh:dh + 1]
            yb_ref[:, sl] = (o * _silu(az_ref[:, sl])).astype(yb_ref.dtype)


def _attn(proj, q, ktb, vtb, km, nbatch):
    n = proj.shape[0]
    t = n // nbatch
    nt = t // MOBA_BLOCK
    assert nt % SUB == 0 and nt <= A_DH
    pairs = [(i, j) for i in range(nt) for j in ([i] + list(range(i)))]
    qi_tab = jnp.asarray([a for a, _ in pairs], jnp.int32)
    kb_tab = jnp.asarray([b for _, b in pairs], jnp.int32)
    blk = (MOBA_BLOCK, A_INNER)
    tspec = pl.BlockSpec((None, A_INNER, MOBA_BLOCK), lambda b, p, qt, kt: (b, 0, kt[p]))
    grid_spec = pltpu.PrefetchScalarGridSpec(
        num_scalar_prefetch=2,
        grid=(nbatch, len(pairs)),
        in_specs=[pl.BlockSpec(blk, lambda b, p, qt, kt: (b * nt + qt[p], 0)),
                  tspec, tspec,
                  pl.BlockSpec((None, nt, A_INNER), lambda b, p, qt, kt: (b, 0, 0)),
                  pl.BlockSpec(blk, lambda b, p, qt, kt: (b * nt + qt[p], COL_AZ))],
        out_specs=pl.BlockSpec(blk, lambda b, p, qt, kt: (b * nt + qt[p], 0)),
        scratch_shapes=[pltpu.VMEM((A_HEADS, MOBA_BLOCK, 2 * A_DH), BF16),
                        pltpu.VMEM((MOBA_BLOCK, LANE), F32),
                        pltpu.VMEM((A_HEADS, MOBA_BLOCK, 2 * A_DH), F32)],
    )
    return pl.pallas_call(
        _attn_kernel,
        grid_spec=grid_spec,
        out_shape=jax.ShapeDtypeStruct((n, A_INNER), BF16),
        compiler_params=_cparams(("arbitrary", "arbitrary")),
        name="moba_attn",
    )(qi_tab, kb_tab, q, ktb, vtb, km, proj)


def _attn_dec_kernel(npages, tvalid, pt_ref, aq_ref, ak_ref, av_ref, az_ref, cos_ref, sin_ref, *rest):
    k_pages = rest[:npages]
    v_pages = rest[npages:2 * npages]
    yb_ref, kn_ref, o_ref = rest[2 * npages:]
    rows = aq_ref.shape[0]
    page = k_pages[0].shape[1]
    per_blk = MOBA_BLOCK // page
    nblk = npages // per_blk
    cos = cos_ref[...]
    sin = sin_ref[...]
    q = _rope(aq_ref[...], cos, sin) * (A_DH ** -0.5)
    kn = _rope(ak_ref[...], cos, sin)
    kn_ref[...] = kn
    rid = lax.broadcasted_iota(jnp.int32, (rows, A_INNER), 0)
    zpad = jnp.zeros((LANE - rows, A_INNER), F32)
    knp = jnp.concatenate([jnp.where(rid < tvalid, kn, 0.0), zpad], axis=0).astype(BF16)
    vnp = jnp.concatenate([jnp.where(rid < tvalid, av_ref[...], 0.0), zpad], axis=0).astype(BF16)

    bid = lax.broadcasted_iota(jnp.int32, (A_INNER, LANE), 1)
    kmt = jnp.zeros((A_INNER, LANE), F32)
    for n_ in range(nblk):
        tot = jnp.sum(k_pages[n_ * per_blk][...], axis=1, keepdims=True)
        for j in range(1, per_blk):
            tot = tot + jnp.sum(k_pages[n_ * per_blk + j][...], axis=1, keepdims=True)
        kmt = jnp.where(bid == n_, tot * (1.0 / MOBA_BLOCK), kmt)

    lane_i = lax.broadcasted_iota(jnp.int32, (rows, LANE), 1)
    lane = lane_i.astype(F32)
    qrow = lax.broadcasted_iota(jnp.int32, (rows, LANE), 0)
    own_ok = lane_i <= qrow
    valid = lane_i < nblk
    qb = q.astype(BF16)
    for h in range(A_HEADS):
        sl = slice(h * A_DH, (h + 1) * A_DH)
        bs = _dot_hi(q[:, sl], kmt[sl, :], _dot)
        s_ = jnp.where(valid, bs, -jnp.inf)
        bias_all = jnp.full(bs.shape, NEG, F32)
        for _ in range(min(MOBA_TOPK, nblk)):
            mx = jnp.max(s_, axis=1, keepdims=True)
            first = jnp.min(jnp.where(s_ == mx, lane, float(LANE)), axis=1, keepdims=True)
            hit = lane == first
            bias_all = jnp.where(jnp.logical_and(hit, valid), 0.0, bias_all)
            s_ = jnp.where(hit, -jnp.inf, s_)
        qh = qb[:, sl]
        parts = []
        for j in range(npages):
            n_ = j // per_blk
            sc = _dot(qh, k_pages[j][sl, :].astype(BF16))
            parts.append(sc + bias_all[:, n_:n_ + 1])
        parts.append(jnp.where(own_ok, _dot_nt(qh, knp[:, sl]), NEG))
        mx = parts[0].max(axis=1, keepdims=True)
        for sc in parts[1:]:
            mx = jnp.maximum(mx, sc.max(axis=1, keepdims=True))
        den = jnp.zeros((rows, 1), F32)
        acc = jnp.zeros((rows, A_DH), F32)
        for j, sc in enumerate(parts):
            pr = jnp.exp(sc - mx)
            den = den + jnp.sum(pr, axis=1, keepdims=True)
            if j < npages:
                acc = acc + _dot_nt(pr.astype(BF16), v_pages[j][sl, :].astype(BF16))
            else:
                acc = acc + _dot(pr.astype(BF16), vnp[:, sl])
        o_ref[:, sl] = acc / den
    yb_ref[...] = (o_ref[...] * _silu(az_ref[...])).astype(yb_ref.dtype)


def _attn_dec(proj, cache_k4, cache_v4, page_table, cos, sin, layer, rows, tvalid):
    n = proj.shape[0]
    nbatch = n // rows
    npages = page_table.shape[1]
    page = cache_k4.shape[3]
    blk = (rows, A_INNER)
    rowspec = lambda col: pl.BlockSpec(blk, lambda b, pt: (b, col))
    const = pl.BlockSpec(blk, lambda b, pt: (0, 0))
    page_specs = [pl.BlockSpec((None, None, A_INNER, page),
                               functools.partial(lambda j, b, pt: (layer, pt[b, j], 0, 0), j))
                  for j in range(npages)]
    grid_spec = pltpu.PrefetchScalarGridSpec(
        num_scalar_prefetch=1,
        grid=(nbatch,),
        in_specs=[rowspec(COL_AQ), rowspec(COL_AK), rowspec(COL_AV), rowspec(COL_AZ), const, const]
                 + page_specs + page_specs,
        out_specs=[pl.BlockSpec(blk, lambda b, pt: (b, 0)), pl.BlockSpec(blk, lambda b, pt: (b, 0))],
        scratch_shapes=[pltpu.VMEM(blk, F32)],
    )
    return pl.pallas_call(
        functools.partial(_attn_dec_kernel, npages, tvalid),
        grid_spec=grid_spec,
        out_shape=[jax.ShapeDtypeStruct((n, A_INNER), BF16), jax.ShapeDtypeStruct((n, A_INNER), F32)],
        compiler_params=_cparams(("arbitrary",)),
        name="moba_decode",
    )(page_table, proj, proj, proj, proj, cos, sin, *([cache_k4] * npages), *([cache_v4] * npages))


def _merge_kernel(alpha, x_ref, ga_ref, gb_ref, gc_ref, ya_ref, yb_ref, yc_ref,
                  wa_ref, wb_ref, wc_ref, wo_ref, lg_ref, lb_ref, o_ref):
    merged = (jax.nn.sigmoid(ga_ref[...]) * _dot(ya_ref[...], wa_ref[...])
              + jax.nn.sigmoid(gb_ref[...]) * _dot(yb_ref[...], wb_ref[...])
              + jax.nn.sigmoid(gc_ref[...]) * _dot(yc_ref[...], wc_ref[...]))
    out = _dot(merged.astype(BF16), wo_ref[...])
    y = alpha * x_ref[...] + out
    mu = jnp.mean(y, axis=1, keepdims=True)
    yc = y - mu
    var = jnp.mean(yc * yc, axis=1, keepdims=True)
    o_ref[...] = yc * lax.rsqrt(var + LN_EPS) * lg_ref[...] + lb_ref[...]


def _merge(x, proj, ya, yb, yc, wl, alpha, tm):
    n = x.shape[0]
    rowblk = lambda w, col: pl.BlockSpec((tm, w), lambda i: (i, col))
    full = lambda shp: pl.BlockSpec(shp, lambda i: (0,) * len(shp))
    return pl.pallas_call(
        functools.partial(_merge_kernel, alpha),
        grid=(n // tm,),
        in_specs=[rowblk(D_MODEL, 0), rowblk(D_MODEL, 0), rowblk(D_MODEL, 1), rowblk(D_MODEL, 2),
                  rowblk(M_INNER, 0), rowblk(A_INNER, 0), rowblk(S_INNER, 0),
                  full((M_INNER, D_MODEL)), full((A_INNER, D_MODEL)), full((S_INNER, D_MODEL)),
                  full((D_MODEL, D_MODEL)), full((1, D_MODEL)), full((1, D_MODEL))],
        out_specs=rowblk(D_MODEL, 0),
        out_shape=jax.ShapeDtypeStruct((n, D_MODEL), F32),
        compiler_params=_cparams(("arbitrary",)),
        name="merge_ln",
    )(x, proj, proj, proj, ya, yb, yc, wl["w_ba"], wl["w_bb"], wl["w_bc"], wl["w_out"], wl["ln_g"], wl["ln_b"])


def _rope_tables(pos):
    half = ROPE_DIMS // 2
    inv = jnp.power(ROPE_THETA, -jnp.arange(half, dtype=F32) * 2.0 / ROPE_DIMS)
    ang = pos.astype(F32)[:, None] * inv[None, :]
    cos, sin = jnp.cos(ang), jnp.sin(ang)
    ones = jnp.ones((pos.shape[0], A_DH - ROPE_DIMS), F32)
    cos_h = jnp.concatenate([cos, cos, ones], axis=1)
    sin_h = jnp.concatenate([-sin, sin, jnp.zeros_like(ones)], axis=1)
    return jnp.tile(cos_h, (1, A_HEADS)), jnp.tile(sin_h, (1, A_HEADS))


def _prep_weights(w_in, mlstm_conv_w, mlstm_conv_b, mlstm_wq, mlstm_wk, mlstm_wv, mlstm_b_i, mlstm_b_f,
                  mlstm_norm_w, mlstm_skip, ssm_conv_w, ssm_conv_b, ssm_dt_bias, ssm_a_log, ssm_d, ssm_norm_w,
                  w_branch_a, w_branch_b, w_branch_c, w_out, ln_g, ln_b):
    depth = w_in.shape[0]
    seg = {name: (off, off + width) for name, off, width in zip(
        ("m_x", "m_z", "m_i", "m_f", "a_q", "a_k", "a_v", "a_z", "s_xbc", "s_z", "s_dt", "gates"),
        IN_OFFSETS, IN_SPLITS)}
    cols = lambda name: w_in[:, :, seg[name][0]:seg[name][1]]
    w_main = jnp.concatenate([cols(nm) for nm in ("gates", "s_xbc", "m_x", "m_z", "a_q", "a_k", "a_v", "a_z", "s_z")],
                             axis=2).astype(BF16)
    pad = SMALL_W - 2 * M_HEADS - S_HEADS
    w_small = jnp.concatenate([cols("m_i"), cols("m_f"), cols("s_dt"),
                               jnp.zeros((depth, D_MODEL, pad), F32)], axis=2).astype(BF16)
    gate_bias = jnp.concatenate([mlstm_b_i, mlstm_b_f, ssm_dt_bias, jnp.zeros((depth, pad), F32)], axis=1)
    alog = jnp.concatenate([jnp.zeros((depth, 2 * M_HEADS), F32), ssm_a_log, jnp.zeros((depth, pad), F32)], axis=1)
    return {
        "w_main": w_main, "w_small": w_small,
        "gate_bias": gate_bias[:, None, :], "s_alog": alog[:, None, :],
        "m_cw": mlstm_conv_w, "m_cb": mlstm_conv_b[:, None, :],
        "m_wqk": jnp.concatenate([mlstm_wq, mlstm_wk], axis=3).astype(BF16),
        "m_wv": mlstm_wv.astype(BF16),
        "m_nw": mlstm_norm_w[:, None, :], "m_sk": mlstm_skip[:, None, :],
        "s_cw": ssm_conv_w, "s_cb": ssm_conv_b[:, None, :],
        "s_dskip": jnp.repeat(ssm_d, S_DH, axis=1)[:, None, :], "s_nw": ssm_norm_w[:, None, :],
        "w_ba": w_branch_a.astype(BF16), "w_bb": w_branch_b.astype(BF16), "w_bc": w_branch_c.astype(BF16),
        "w_out": w_out.astype(BF16), "ln_g": ln_g[:, None, :], "ln_b": ln_b[:, None, :],
    }


def _pad_lanes(m):
    return jnp.pad(m, ((0, 0), (0, LANE - m.shape[1])))[:, None, :]


def kernel(x_prompt, x_sample, cache_k, cache_v, page_table, state_mlstm_c, state_mlstm_n, state_mlstm_m,
           state_mlstm_conv, state_ssm, state_ssm_conv, w_in, mlstm_conv_w, mlstm_conv_b, mlstm_wq, mlstm_wk,
           mlstm_wv, mlstm_b_i, mlstm_b_f, mlstm_norm_w, mlstm_skip, ssm_conv_w, ssm_conv_b, ssm_dt_bias,
           ssm_a_log, ssm_d, ssm_norm_w, w_branch_a, w_branch_b, w_branch_c, w_out, ln_g, ln_b):
    depth = w_in.shape[0]
    bp, tp, _ = x_prompt.shape
    bs, ts, _ = x_sample.shape
    npool, page = cache_k.shape[1], cache_k.shape[2]
    n_past = page_table.shape[1] * page
    assert tp % MOBA_BLOCK == 0 and n_past % MOBA_BLOCK == 0 and MOBA_BLOCK % page == 0
    assert CONV_W - 1 <= ts <= SUB
    alpha = float((2 * depth) ** DEPTH_ALPHA_POW)
    rows = SUB
    chunk = min(256, tp)

    w = _prep_weights(w_in, mlstm_conv_w, mlstm_conv_b, mlstm_wq, mlstm_wk, mlstm_wv, mlstm_b_i, mlstm_b_f,
                      mlstm_norm_w, mlstm_skip, ssm_conv_w, ssm_conv_b, ssm_dt_bias, ssm_a_log, ssm_d, ssm_norm_w,
                      w_branch_a, w_branch_b, w_branch_c, w_out, ln_g, ln_b)
    cos_p, sin_p = _rope_tables(jnp.arange(tp, dtype=jnp.int32))
    cos_s, sin_s = _rope_tables(n_past + jnp.arange(rows, dtype=jnp.int32))
    cache_k4 = jnp.transpose(cache_k, (0, 1, 3, 4, 2)).reshape(depth, npool, A_INNER, page)
    cache_v4 = jnp.transpose(cache_v, (0, 1, 3, 4, 2)).reshape(depth, npool, A_INNER, page)

    xp = x_prompt.reshape(bp * tp, D_MODEL)
    xs = jnp.pad(x_sample, ((0, 0), (0, rows - ts), (0, 0))).reshape(bs * rows, D_MODEL)
    zeros_p = dict(conv=jnp.zeros((bp, CONV_W - 1, M_INNER), F32), c=jnp.zeros((bp, M_HEADS, M_DH, M_DH), F32),
                   n=jnp.zeros((bp, M_HEADS, M_DH), F32), m=jnp.zeros((bp, 1, LANE), F32),
                   s=jnp.zeros((bp, S_HEADS, S_DH, S_STATE), F32), sconv=jnp.zeros((bp, CONV_W - 1, S_CONV_DIM), F32))
    tm_p = min(1024, bp * tp)
    new_p, new_s = [], []
    for l in range(depth):
        wl = {k_: v_[l] for k_, v_ in w.items()}
        proj, small = _proj(xp, wl["w_main"], wl["w_small"], tm_p, 1280)
        ya, mc, mn, mm, mconv = _mlstm(proj, small, zeros_p["conv"], zeros_p["c"], zeros_p["n"], zeros_p["m"],
                                       wl, bp, chunk, chunk)
        yc, sst, sconv = _ssd(proj, small, zeros_p["sconv"], zeros_p["s"], wl, bp, chunk, chunk)
        q, kt, vt, ktb, vtb, km = _attn_prep(proj, cos_p, sin_p, bp)
        yb = _attn(proj, q, ktb, vtb, km, bp)
        xp = _merge(xp, proj, ya, yb, yc, wl, alpha, min(512, bp * tp))
        new_p.append((kt, vt, mc, mn, mm[:, 0, :M_HEADS], mconv, sst, sconv))
        proj, small = _proj(xs, wl["w_main"], wl["w_small"], bs * rows, 1280)
        ya, mc, mn, mm, mconv = _mlstm(proj, small, state_mlstm_conv[l], state_mlstm_c[l], state_mlstm_n[l],
                                       _pad_lanes(state_mlstm_m[l]), wl, bs, rows, ts)
        yc, sst, sconv = _ssd(proj, small, state_ssm_conv[l], state_ssm[l], wl, bs, rows, ts)
        yb, knew = _attn_dec(proj, cache_k4, cache_v4, page_table, cos_s, sin_s, l, rows, ts)
        v_new = proj[:, COL_AV * A_INNER:(COL_AV + 1) * A_INNER]
        xs = _merge(xs, proj, ya, yb, yc, wl, alpha, min(512, bs * rows))
        new_s.append((knew.reshape(bs, rows, A_HEADS, A_DH)[:, :ts], v_new.reshape(bs, rows, A_HEADS, A_DH)[:, :ts],
                      mc, mn, mm[:, 0, :M_HEADS], mconv, sst, sconv))
    outs_p = [jnp.stack(a, axis=0) for a in zip(*new_p)]
    outs_p[0] = jnp.transpose(outs_p[0].reshape(depth, bp, A_HEADS, A_DH, tp), (0, 1, 4, 2, 3))
    outs_p[1] = jnp.transpose(outs_p[1].reshape(depth, bp, A_HEADS, A_DH, tp), (0, 1, 4, 2, 3))
    outs_s = [jnp.stack(a, axis=0) for a in zip(*new_s)]
    y_p = xp.reshape(bp, tp, D_MODEL)
    y_s = xs.reshape(bs, rows, D_MODEL)[:, :ts]
    return (y_p, y_s, *outs_p, *outs_s)
```
